```python
import jax, jax.numpy as jnp
from jax import lax
import numpy as np

D_MODEL = 1024
BATCH = 16
SEQ = 2048
DEPTH = 4

CHUNK = 64
N_MIXERS = 2
N_CONV_LAYERS = (DEPTH + 1) // 2
N_SSD_LAYERS = DEPTH // 2
N_MOD = 6
EPS = 1e-6

SC_WIDTH = 3

M_EXPAND = 2
M_D_INNER = M_EXPAND * D_MODEL
M_HEAD_DIM = 64
M_N_HEADS = M_D_INNER // M_HEAD_DIM
M_N_GROUPS = 8
M_HEADS_PER_GROUP = M_N_HEADS // M_N_GROUPS
M_D_STATE = 128
M_CONV_WIDTH = 4
M_CONV_DIM = M_D_INNER + 2 * M_N_GROUPS * M_D_STATE
M_IN_DIM = M_D_INNER + M_CONV_DIM + M_N_HEADS
SSD_CHUNK = CHUNK

D_FF = -(-8 * D_MODEL // (3 * 256)) * 256

kernel_name = "hybrid_shortconv_ssd_streaming_trunk"


def rms_normalize(x):
    xf = x.astype(jnp.float32)
    xf = xf * lax.rsqrt(jnp.mean(xf * xf, axis=-1, keepdims=True) + EPS)
    return xf.astype(x.dtype)


def rmsnorm(x, g):
    return rms_normalize(x) * g


def causal_depthwise_conv(x, w):
    k_width = w.shape[0]
    s = x.shape[1]
    xp = jnp.pad(x, ((0, 0), (k_width - 1, 0), (0, 0)))
    y = w[0] * xp[:, 0:s]
    for k in range(1, k_width):
        y = y + w[k] * xp[:, k:k + s]
    return y


def short_conv_mixer(h, w_in, conv_w, w_out):
    b_gate, c_gate, v = jnp.split(h @ w_in, 3, axis=-1)
    u = causal_depthwise_conv(c_gate * v, conv_w)
    return (b_gate * u) @ w_out


def ssd_chunked(xh, dt, a, bm, cm):
    bsz, s, g, r, p = xh.shape
    n = bm.shape[-1]
    nc, l = s // SSD_CHUNK, SSD_CHUNK
    dtype = xh.dtype
    xdt = (xh * dt[..., None].astype(dtype)).reshape(bsz, nc, l, g, r, p)
    bm = bm.reshape(bsz, nc, l, g, n)
    cm = cm.reshape(bsz, nc, l, g, n)
    cs = jnp.cumsum((dt * a).reshape(bsz, nc, l, g, r), axis=2)
    causal = jnp.tril(jnp.ones((l, l), dtype=bool))[None, None, :, :, None, None]
    seg = cs[:, :, :, None] - cs[:, :, None]
    decay = jnp.exp(jnp.where(causal, seg, -jnp.inf)).astype(dtype)
    cb = jnp.einsum('bclgn,bcsgn->bclsg', cm, bm)
    y_diag = jnp.einsum('bclsgr,bcsgrp->bclgrp', cb[..., None] * decay, xdt)
    decay_to_end = jnp.exp(cs[:, :, -1:] - cs).astype(dtype)
    states = jnp.einsum('bclgn,bclgr,bclgrp->bcgrpn', bm, decay_to_end, xdt)
    chunk_decay = jnp.exp(cs[:, :, -1]).astype(dtype)

    def step(state, inp):
        st, dec = inp
        return state * dec[..., None, None] + st, state

    h0 = jnp.zeros((bsz, g, r, p, n), dtype)
    _, prev = lax.scan(step, h0, (jnp.moveaxis(states, 1, 0), jnp.moveaxis(chunk_decay, 1, 0)))
    prev = jnp.moveaxis(prev, 0, 1)
    y_off = jnp.einsum('bclgn,bcgrpn,bclgr->bclgrp', cm, prev, jnp.exp(cs).astype(dtype))
    return (y_diag + y_off).reshape(bsz, s, g, r, p)


def ssd_mixer(h, w_in, conv_w, conv_b, dt_bias, a_log, d_skip, norm_g, w_out):
    bsz, s, _ = h.shape
    g, r, p, n = M_N_GROUPS, M_HEADS_PER_GROUP, M_HEAD_DIM, M_D_STATE
    z, xbc, dt_raw = jnp.split(h @ w_in, [M_D_INNER, M_D_INNER + M_CONV_DIM], axis=-1)
    xbc = jax.nn.silu(causal_depthwise_conv(xbc, conv_w) + conv_b)
    xs, bm, cm = jnp.split(xbc, [M_D_INNER, M_D_INNER + g * n], axis=-1)
    xs = xs.reshape(bsz, s, g, r, p)
    bm = bm.reshape(bsz, s, g, n)
    cm = cm.reshape(bsz, s, g, n)
    dt = jax.nn.softplus(dt_raw.astype(jnp.float32) + dt_bias.astype(jnp.float32)).reshape(bsz, s, g, r)
    a = -jnp.exp(a_log.astype(jnp.float32)).reshape(g, r)
    y = ssd_chunked(xs, dt, a, bm, cm) + d_skip.reshape(g, r)[:, :, None] * xs
    y = y.reshape(bsz, s, M_D_INNER) * jax.nn.silu(z)
    y = rms_normalize(y.reshape(bsz, s, g, M_D_INNER // g)).reshape(bsz, s, M_D_INNER) * norm_g
    return y @ w_out


def swiglu_ffn(h, w_in, w_out):
    gate, up = jnp.split(h @ w_in, 2, axis=-1)
    return (jax.nn.silu(gate) * up) @ w_out


def setup_inputs(seed: int = 0) -> dict:
    key = jax.random.key(seed)
    ks = jax.random.split(key, 20)
    f32 = jnp.float32
    nrm = lambda k, shape, scale: jax.random.normal(k, shape, f32) * scale
    dt_init = jnp.exp(jax.random.uniform(ks[11], (N_SSD_LAYERS, M_N_HEADS), f32,
                                         np.log(1e-3), np.log(1e-1)))
    return {
        "x": nrm(ks[0], (BATCH, SEQ, D_MODEL), 1.0),
        "c": nrm(ks[1], (BATCH, D_MODEL), 1.0),
        "ada_w": nrm(ks[2], (D_MODEL, DEPTH * N_MOD * D_MODEL), 0.5 * D_MODEL ** -0.5),
        "ada_b": nrm(ks[3], (DEPTH * N_MOD * D_MODEL,), 0.02),
        "norm_g": 1.0 + nrm(ks[4], (DEPTH, 4, D_MODEL), 0.02),
        "a_w_in": nrm(ks[5], (N_CONV_LAYERS, D_MODEL, 3 * D_MODEL), D_MODEL ** -0.5),
        "a_conv_w": nrm(ks[6], (N_CONV_LAYERS, SC_WIDTH, D_MODEL), SC_WIDTH ** -0.5),
        "a_w_out": nrm(ks[7], (N_CONV_LAYERS, D_MODEL, D_MODEL), D_MODEL ** -0.5),
        "m_w_in": nrm(ks[8], (N_SSD_LAYERS, D_MODEL, M_IN_DIM), D_MODEL ** -0.5),
        "m_conv_w": nrm(ks[9], (N_SSD_LAYERS, M_CONV_WIDTH, M_CONV_DIM), M_CONV_WIDTH ** -0.5),
        "m_conv_b": nrm(ks[10], (N_SSD_LAYERS, M_CONV_DIM), 0.02),
        "m_dt_bias": dt_init + jnp.log(-jnp.expm1(-dt_init)),
        "m_a_log": jnp.log(jax.random.uniform(ks[12], (N_SSD_LAYERS, M_N_HEADS), f32, 1.0, 16.0)),
        "m_d": 1.0 + nrm(ks[13], (N_SSD_LAYERS, M_N_HEADS), 0.1),
        "m_norm_g": 1.0 + nrm(ks[14], (N_SSD_LAYERS, M_D_INNER), 0.02),
        "m_w_out": nrm(ks[15], (N_SSD_LAYERS, M_D_INNER, D_MODEL), M_D_INNER ** -0.5),
        "f_w_in": nrm(ks[16], (DEPTH, D_MODEL, 2 * D_FF), D_MODEL ** -0.5),
        "f_w_out": nrm(ks[17], (DEPTH, D_FF, D_MODEL), D_FF ** -0.5),
    }


def reference(x, c, ada_w, ada_b, norm_g, a_w_in, a_conv_w, a_w_out, m_w_in, m_conv_w,
              m_conv_b, m_dt_bias, m_a_log, m_d, m_norm_g, m_w_out, f_w_in, f_w_out):
    bsz = x.shape[0]
    mod = (jax.nn.silu(c) @ ada_w + ada_b).reshape(bsz, DEPTH, 2, 3, D_MODEL)
    for i in range(DEPTH):
        j = i // N_MIXERS
        shift, scale, gate = (mod[:, i, 0, k][:, None, :] for k in range(3))
        h = rmsnorm(x, norm_g[i, 0]) * (1.0 + scale) + shift
        if i % N_MIXERS == 0:
            y = short_conv_mixer(h, a_w_in[j], a_conv_w[j], a_w_out[j])
        else:
            y = ssd_mixer(h, m_w_in[j], m_conv_w[j], m_conv_b[j], m_dt_bias[j], m_a_log[j],
                          m_d[j], m_norm_g[j], m_w_out[j])
        x = x + gate * rmsnorm(y, norm_g[i, 1])
        shift, scale, gate = (mod[:, i, 1, k][:, None, :] for k in range(3))
        h = rmsnorm(x, norm_g[i, 2]) * (1.0 + scale) + shift
        y = swiglu_ffn(h, f_w_in[i], f_w_out[i])
        x = x + gate * rmsnorm(y, norm_g[i, 3])
    return x
```

```python
import functools

import jax
import jax.numpy as jnp
from jax import lax
from jax.experimental import pallas as pl
from jax.experimental.pallas import tpu as pltpu

F32 = jnp.float32
BF16 = jnp.bfloat16

EPS = 1e-6
SSD_CHUNK = 64
HEAD_DIM = 64
HEADS_PER_GROUP = 4
D_STATE = 128
N_GROUPS = 8
N_HEADS = N_GROUPS * HEADS_PER_GROUP
GROUP_WIDTH = HEADS_PER_GROUP * HEAD_DIM
LANES = 128
HALO = 8
VMEM_LIMIT = 60 * 1024 * 1024


def _sigmoid(x):
    return 0.5 * jnp.tanh(0.5 * x) + 0.5


def _silu(x):
    return x * _sigmoid(x)


def _rms_normalize(x):
    return x * lax.rsqrt(jnp.mean(x * x, axis=-1, keepdims=True) + EPS)


def _prenorm(x, g, mod):
    return _rms_normalize(x) * (g * (1.0 + mod[1:2])) + mod[0:1]


def _residual(x, y, g, mod):
    return x + mod[2:3] * (_rms_normalize(y) * g)


def _dot(a, b):
    return jnp.dot(a, b, preferred_element_type=F32)


def _split3(v):
    hi = v.astype(BF16).astype(F32)
    r = v - hi
    mid = r.astype(BF16).astype(F32)
    lo = (r - mid).astype(BF16).astype(F32)
    return hi, mid, lo


def _mod_kernel(c_ref, w_ref, b_ref, o_ref):
    sc = _silu(c_ref[...]).astype(BF16)
    o_ref[...] = _dot(sc, w_ref[...].astype(BF16)) + b_ref[...]


def _mod_call(c, ada_w, ada_b, tn=2048):
    bsz, d = c.shape
    n = ada_w.shape[1]
    return pl.pallas_call(
        _mod_kernel,
        grid=(n // tn,),
        in_specs=[
            pl.BlockSpec((bsz, d), lambda j: (0, 0)),
            pl.BlockSpec((d, tn), lambda j: (0, j)),
            pl.BlockSpec((1, tn), lambda j: (0, j)),
        ],
        out_specs=pl.BlockSpec((bsz, tn), lambda j: (0, j)),
        out_shape=jax.ShapeDtypeStruct((bsz, n), F32),
        compiler_params=pltpu.CompilerParams(
            dimension_semantics=("arbitrary",), vmem_limit_bytes=VMEM_LIMIT),
        name="mod_proj",
    )(c, ada_w, ada_b.reshape(1, n))


def _ffn_kernel(x_ref, mod_ref, ng_ref, wg_ref, wu_ref, wo_ref, o_ref):
    x = x_ref[...]
    mod = mod_ref[0, 0]
    ng = ng_ref[...]
    h = _prenorm(x, ng[2:3], mod).astype(BF16)
    act = (_silu(_dot(h, wg_ref[...])) * _dot(h, wu_ref[...])).astype(BF16)
    y = _dot(act, wo_ref[...])
    o_ref[...] = _residual(x, y, ng[3:4], mod)


def _const_spec(shape):
    nd = len(shape)
    return pl.BlockSpec(shape, lambda t: (0,) * nd, pipeline_mode=pl.Buffered(1))


def _mod_spec(d, tiles_per_seq, sub):
    return pl.BlockSpec((1, 1, 3, d), lambda t: (t // tiles_per_seq, sub, 0, 0))


def _ffn_call(x, mod, ng, wg, wu, wo, sub, seq, tm):
    tokens, d = x.shape
    dff = wg.shape[1]
    tiles_per_seq = seq // tm
    return pl.pallas_call(
        _ffn_kernel,
        grid=(tokens // tm,),
        in_specs=[
            pl.BlockSpec((tm, d), lambda t: (t, 0)),
            _mod_spec(d, tiles_per_seq, sub),
            _const_spec((4, d)),
            _const_spec((d, dff)),
            _const_spec((d, dff)),
            _const_spec((dff, d)),
        ],
        out_specs=pl.BlockSpec((tm, d), lambda t: (t, 0)),
        out_shape=jax.ShapeDtypeStruct((tokens, d), F32),
        compiler_params=pltpu.CompilerParams(
            dimension_semantics=("arbitrary",), vmem_limit_bytes=VMEM_LIMIT),
        name="ffn",
    )(x, mod, ng, wg, wu, wo)


def _shifted(ext, k):
    return pltpu.roll(ext, k, axis=0)[HALO:]


def _conv_mixer_kernel(x_ref, mod_ref, ng_ref, win_ref, cw_ref, wout_ref, o_ref,
                       halo_ref, *, tiles_per_seq):
    tm, d = x_ref.shape

    @pl.when(pl.program_id(0) % tiles_per_seq == 0)
    def _():
        halo_ref[...] = jnp.zeros_like(halo_ref)

    x = x_ref[...]
    mod = mod_ref[0, 0]
    ng = ng_ref[...]
    h = _prenorm(x, ng[0:1], mod).astype(BF16)
    bcv = _dot(h, win_ref[...])
    b_gate = bcv[:, 0:d]
    cv = bcv[:, d:2 * d] * bcv[:, 2 * d:3 * d]
    ext = jnp.concatenate([halo_ref[...], cv], axis=0)
    halo_ref[...] = cv[tm - HALO:tm]
    cw = cw_ref[...]
    u = cw[2:3] * cv + cw[1:2] * _shifted(ext, 1) + cw[0:1] * _shifted(ext, 2)
    y = _dot((b_gate * u).astype(BF16), wout_ref[...])
    o_ref[...] = _residual(x, y, ng[1:2], mod)


def _conv_mixer_call(x, mod, ng, win, cw, wout, sub, seq, tm):
    tokens, d = x.shape
    tiles_per_seq = seq // tm
    return pl.pallas_call(
        functools.partial(_conv_mixer_kernel, tiles_per_seq=tiles_per_seq),
        grid=(tokens // tm,),
        in_specs=[
            pl.BlockSpec((tm, d), lambda t: (t, 0)),
            _mod_spec(d, tiles_per_seq, sub),
            _const_spec((4, d)),
            _const_spec(win.shape),
            _const_spec(cw.shape),
            _const_spec(wout.shape),
        ],
        out_specs=pl.BlockSpec((tm, d), lambda t: (t, 0)),
        out_shape=jax.ShapeDtypeStruct((tokens, d), F32),
        scratch_shapes=[pltpu.VMEM((HALO, d), F32)],
        compiler_params=pltpu.CompilerParams(
            dimension_semantics=("arbitrary",), vmem_limit_bytes=VMEM_LIMIT),
        name="conv_mixer",
    )(x, mod, ng, win, cw, wout)


def _ssd_mixer_kernel(x_ref, mod_ref, ng_ref, wz_ref, wxbc_ref, wdt_ref, cw_ref, cb_ref,
                      dtb_ref, alog_ref, dskip_ref, mg_ref, expand_ref, wout_ref, o_ref,
                      halo_ref, state_ref, xbc_ref, z_ref, dt_ref, y_ref, *, tiles_per_seq):
    tm, d = x_ref.shape
    d_inner = z_ref.shape[1]
    gn = N_GROUPS * D_STATE
    lc = SSD_CHUNK

    @pl.when(pl.program_id(0) % tiles_per_seq == 0)
    def _():
        halo_ref[...] = jnp.zeros_like(halo_ref)
        state_ref[...] = jnp.zeros_like(state_ref)

    x = x_ref[...]
    mod = mod_ref[0, 0]
    ng = ng_ref[...]
    h = _prenorm(x, ng[0:1], mod).astype(BF16)

    z_ref[...] = _dot(h, wz_ref[...])
    raw = _dot(h, wxbc_ref[...])
    head_lane = lax.broadcasted_iota(jnp.int32, (1, LANES), 1) < N_HEADS
    dt_raw = _dot(h, wdt_ref[...]) + dtb_ref[...]
    softplus = jnp.maximum(dt_raw, 0.0) + jnp.log1p(jnp.exp(-jnp.abs(dt_raw)))
    dt_ref[...] = jnp.where(head_lane, softplus, 0.0)

    ext = jnp.concatenate([halo_ref[...], raw], axis=0)
    halo_ref[...] = raw[tm - HALO:tm]
    cw = cw_ref[...]
    conv = (cw[3:4] * raw + cw[2:3] * _shifted(ext, 1) + cw[1:2] * _shifted(ext, 2)
            + cw[0:1] * _shifted(ext, 3) + cb_ref[...])
    xbc_ref[...] = _silu(conv)

    a_neg = jnp.where(head_lane, -jnp.exp(alog_ref[...]), 0.0)
    row = lax.broadcasted_iota(jnp.int32, (lc, lc), 0)
    col = lax.broadcasted_iota(jnp.int32, (lc, lc), 1)
    tril = (row >= col).astype(BF16)
    tril3 = jnp.concatenate([tril, tril, tril], axis=1)
    row_w = lax.broadcasted_iota(jnp.int32, (lc, d_inner), 0)
    pos_w = lax.broadcasted_iota(jnp.int32, (lc, d_inner), 1) % HEAD_DIM
    diag_w = row_w == pos_w
    causal_w = row_w >= pos_w
    head_of_lane = lax.broadcasted_iota(jnp.int32, (lc, GROUP_WIDTH), 1) // HEAD_DIM
    expand = expand_ref[...]
    dskip = dskip_ref[...]
    mg = mg_ref[...]

    def lane_pack(v):
        hi, mid, lo = _split3(v)
        return hi + pltpu.roll(mid, N_HEADS, axis=1) + pltpu.roll(lo, 2 * N_HEADS, axis=1)

    def chunk_body(c, carry):
        rows = pl.ds(pl.multiple_of(c * lc, lc), lc)
        xs = xbc_ref[rows, 0:d_inner]
        bm = xbc_ref[rows, d_inner:d_inner + gn].astype(BF16)
        cm = xbc_ref[rows, d_inner + gn:d_inner + 2 * gn].astype(BF16)
        dt = dt_ref[rows, :]
        da = dt * a_neg
        cs = _dot(tril3, jnp.concatenate(_split3(da), axis=0).astype(BF16))
        packed = jnp.concatenate([lane_pack(cs), lane_pack(dt)], axis=0).astype(BF16)
        wide = _dot(packed, expand)
        cs_w = wide[0:lc]
        dt_w = wide[lc:2 * lc]
        cs_last = cs_w[lc - 1:lc]
        xdt = xs * dt_w
        ecs = jnp.exp(cs_w)
        to_end = (xdt * jnp.exp(cs_last - cs_w)).astype(BF16)
        chunk_decay = jnp.exp(cs_last)
        cs_row = jnp.sum(jnp.where(diag_w, cs_w, 0.0), axis=0, keepdims=True)
        decay = jnp.exp(jnp.where(causal_w, cs_w - cs_row, -jnp.inf))
        for g in range(N_GROUPS):
            gl = slice(g * GROUP_WIDTH, (g + 1) * GROUP_WIDTH)
            nl = slice(g * D_STATE, (g + 1) * D_STATE)
            bm_g = bm[:, nl]
            cm_g = cm[:, nl]
            b4 = jnp.concatenate([bm_g] * HEADS_PER_GROUP, axis=0)
            cb4 = lax.dot_general(cm_g, b4, (((1,), (1,)), ((), ())),
                                  preferred_element_type=F32)
            m_g = (cb4 * decay[:, gl]).astype(BF16)
            xdt_g = xdt[:, gl]
            blockdiag = jnp.concatenate(
                [jnp.where(head_of_lane == r, xdt_g, 0.0) for r in range(HEADS_PER_GROUP)],
                axis=0).astype(BF16)
            state_g = state_ref[:, gl]
            y_g = _dot(m_g, blockdiag) + ecs[:, gl] * _dot(cm_g, state_g.astype(BF16))
            new_state = lax.dot_general(bm_g, to_end[:, gl], (((0,), (0,)), ((), ())),
                                        preferred_element_type=F32)
            state_ref[:, gl] = state_g * chunk_decay[:, gl] + new_state
            y_g = (y_g + dskip[:, gl] * xs[:, gl]) * _silu(z_ref[rows, gl])
            y_ref[rows, gl] = (_rms_normalize(y_g) * mg[:, gl]).astype(BF16)
        return carry

    lax.fori_loop(0, tm // lc, chunk_body, 0)

    y = _dot(y_ref[...], wout_ref[...])
    o_ref[...] = _residual(x, y, ng[1:2], mod)


def _ssd_mixer_call(x, mod, ng, wz, wxbc, wdt, cw, cb, dtb, alog, dskip, mg, expand, wout,
                    sub, seq, tm):
    tokens, d = x.shape
    d_inner = wz.shape[1]
    conv_dim = wxbc.shape[1]
    tiles_per_seq = seq // tm
    consts = (ng, wz, wxbc, wdt, cw, cb, dtb, alog, dskip, mg, expand, wout)
    return pl.pallas_call(
        functools.partial(_ssd_mixer_kernel, tiles_per_seq=tiles_per_seq),
        grid=(tokens // tm,),
        in_specs=[pl.BlockSpec((tm, d), lambda t: (t, 0)), _mod_spec(d, tiles_per_seq, sub)]
        + [_const_spec(a.shape) for a in consts],
        out_specs=pl.BlockSpec((tm, d), lambda t: (t, 0)),
        out_shape=jax.ShapeDtypeStruct((tokens, d), F32),
        scratch_shapes=[
            pltpu.VMEM((HALO, conv_dim), F32),
            pltpu.VMEM((D_STATE, d_inner), F32),
            pltpu.VMEM((tm, conv_dim), F32),
            pltpu.VMEM((tm, d_inner), F32),
            pltpu.VMEM((tm, LANES), F32),
            pltpu.VMEM((tm, d_inner), BF16),
        ],
        compiler_params=pltpu.CompilerParams(
            dimension_semantics=("arbitrary",), vmem_limit_bytes=VMEM_LIMIT),
        name="ssd_mixer",
    )(x, mod, *consts)


def _pad_lanes(a):
    return jnp.pad(a, ((0, 0), (0, LANES - a.shape[1])))


def kernel(x, c, ada_w, ada_b, norm_g, a_w_in, a_conv_w, a_w_out, m_w_in, m_conv_w, m_conv_b,
           m_dt_bias, m_a_log, m_d, m_norm_g, m_w_out, f_w_in, f_w_out):
    bsz, seq, d = x.shape
    depth = norm_g.shape[0]
    d_inner = m_w_out.shape[1]
    conv_dim = m_conv_w.shape[2]
    dff = f_w_out.shape[1]
    tm_ffn = min(512, seq)
    tm_conv = min(512, seq)
    tm_ssd = min(256, seq)

    mod = _mod_call(c, ada_w, ada_b).reshape(bsz, depth * 2, 3, d)
    head_of_col = jnp.arange(d_inner) // HEAD_DIM
    k = jnp.arange(LANES)
    expand = ((k[:, None] % N_HEADS == head_of_col[None, :]) & (k[:, None] < 3 * N_HEADS)
              ).astype(BF16)

    xt = x.reshape(bsz * seq, d)
    for i in range(depth):
        j = i // 2
        if i % 2 == 0:
            xt = _conv_mixer_call(xt, mod, norm_g[i], a_w_in[j].astype(BF16), a_conv_w[j],
                                  a_w_out[j].astype(BF16), 2 * i, seq, tm_conv)
        else:
            w_in = m_w_in[j]
            xt = _ssd_mixer_call(
                xt, mod, norm_g[i],
                w_in[:, 0:d_inner].astype(BF16),
                w_in[:, d_inner:d_inner + conv_dim].astype(BF16),
                _pad_lanes(w_in[:, d_inner + conv_dim:]).astype(BF16),
                m_conv_w[j], m_conv_b[j][None, :],
                _pad_lanes(m_dt_bias[j][None, :]), _pad_lanes(m_a_log[j][None, :]),
                jnp.repeat(m_d[j], HEAD_DIM)[None, :], m_norm_g[j][None, :],
                expand, m_w_out[j].astype(BF16), 2 * i, seq, tm_ssd)
        w_ffn = f_w_in[i]
        xt = _ffn_call(xt, mod, norm_g[i], w_ffn[:, 0:dff].astype(BF16),
                       w_ffn[:, dff:].astype(BF16), f_w_out[i].astype(BF16),
                       2 * i + 1, seq, tm_ffn)
    return xt.reshape(bsz, seq, d)
```

```python
import functools

import jax
import jax.numpy as jnp
from jax import lax
from jax.experimental import pallas as pl
from jax.experimental.pallas import tpu as pltpu

F32 = jnp.float32
BF16 = jnp.bfloat16

EPS = 1e-6
LOG2E = 1.4426950408889634
SSD_CHUNK = 64
HEAD_DIM = 64
HEADS_PER_GROUP = 4
D_STATE = 128
N_GROUPS = 8
N_HEADS = N_GROUPS * HEADS_PER_GROUP
GROUP_WIDTH = HEADS_PER_GROUP * HEAD_DIM
LANES = 128
HALO = 8
COL_BLOCK = 1024
VMEM_LIMIT = 60 * 1024 * 1024


def _silu(x):
    hx = 0.5 * x
    return hx + hx * jnp.tanh(hx)


def _rms_normalize(x):
    return x * lax.rsqrt(jnp.mean(x * x, axis=-1, keepdims=True) + EPS)


def _prenorm(x, g, mod):
    return _rms_normalize(x) * (g * (1.0 + mod[1:2])) + mod[0:1]


def _residual(x, y, g, mod):
    return x + mod[2:3] * (_rms_normalize(y) * g)


def _dot(a, b):
    return jnp.dot(a, b, preferred_element_type=F32)


def _split3(v):
    hi = v.astype(BF16).astype(F32)
    r = v - hi
    mid = r.astype(BF16).astype(F32)
    lo = (r - mid).astype(BF16).astype(F32)
    return hi, mid, lo


def _pad_cols(w):
    return jnp.pad(w.astype(BF16), ((0, 0), (0, LANES)))


def _store_col_tiles(dst_ref, first_tile, vals):
    rows = vals.shape[0]
    for j in range(vals.shape[1] // LANES):
        dst_ref[first_tile + j, HALO:HALO + rows, :] = vals[:, j * LANES:(j + 1) * LANES]


def _causal_conv_tile(src_ref, j, taps, rows):
    k_width = taps.shape[0]
    acc = taps[k_width - 1:k_width] * src_ref[j, HALO:HALO + rows, :]
    for k in range(1, k_width):
        acc = acc + taps[k_width - 1 - k:k_width - k] * src_ref[j, HALO - k:HALO - k + rows, :]
    return acc


def _carry_halo(src_ref, j, rows):
    src_ref[j, 0:HALO, :] = src_ref[j, rows:rows + HALO, :]


def _mod_kernel(c_ref, w_ref, b_ref, o_ref):
    sc = _silu(c_ref[...]).astype(BF16)
    o_ref[...] = _dot(sc, w_ref[...].astype(BF16)) + b_ref[...]


def _mod_call(c, ada_w, ada_b, tn=2048):
    bsz, d = c.shape
    n = ada_w.shape[1]
    return pl.pallas_call(
        _mod_kernel,
        grid=(n // tn,),
        in_specs=[
            pl.BlockSpec((bsz, d), lambda j: (0, 0)),
            pl.BlockSpec((d, tn), lambda j: (0, j)),
            pl.BlockSpec((1, tn), lambda j: (0, j)),
        ],
        out_specs=pl.BlockSpec((bsz, tn), lambda j: (0, j)),
        out_shape=jax.ShapeDtypeStruct((bsz, n), F32),
        compiler_params=pltpu.CompilerParams(
            dimension_semantics=("arbitrary",), vmem_limit_bytes=VMEM_LIMIT),
        name="mod_proj",
    )(c, ada_w, ada_b.reshape(1, n))


def _const_spec(shape):
    nd = len(shape)
    return pl.BlockSpec(shape, lambda t: (0,) * nd, pipeline_mode=pl.Buffered(1))


def _mod_spec(d, tiles_per_seq, sub):
    return pl.BlockSpec((1, 1, 3, d), lambda t: (t // tiles_per_seq, sub, 0, 0))


def _ffn_kernel(x_ref, mod_ref, ng_ref, win_ref, wo_ref, o_ref):
    d = x_ref.shape[1]
    dff = wo_ref.shape[0]
    x = x_ref[...]
    mod = mod_ref[0, 0]
    ng = ng_ref[...]
    h = _prenorm(x, ng[2:3], mod).astype(BF16)
    act = (_silu(_dot(h, win_ref[:, 0:dff])) * _dot(h, win_ref[:, dff:2 * dff])).astype(BF16)
    y = _dot(act, wo_ref[:, 0:d])
    o_ref[...] = _residual(x, y, ng[3:4], mod)


def _ffn_call(x, mod, ng, win, wo, sub, seq, tm):
    tokens, d = x.shape
    tiles_per_seq = seq // tm
    return pl.pallas_call(
        _ffn_kernel,
        grid=(tokens // tm,),
        in_specs=[
            pl.BlockSpec((tm, d), lambda t: (t, 0)),
            _mod_spec(d, tiles_per_seq, sub),
            _const_spec((4, d)),
            _const_spec(win.shape),
            _const_spec(wo.shape),
        ],
        out_specs=pl.BlockSpec((tm, d), lambda t: (t, 0)),
        out_shape=jax.ShapeDtypeStruct((tokens, d), F32),
        compiler_params=pltpu.CompilerParams(
            dimension_semantics=("arbitrary",), vmem_limit_bytes=VMEM_LIMIT),
        name="ffn",
    )(x, mod, ng, win, wo)


def _conv_mixer_kernel(x_ref, mod_ref, ng_ref, win_ref, cw_ref, wout_ref, o_ref,
                       cv_ref, bu_ref, *, tiles_per_seq):
    tm, d = x_ref.shape

    @pl.when(pl.program_id(0) % tiles_per_seq == 0)
    def _():
        cv_ref[:, 0:HALO, :] = jnp.zeros((cv_ref.shape[0], HALO, LANES), F32)

    x = x_ref[...]
    mod = mod_ref[0, 0]
    ng = ng_ref[...]
    h = _prenorm(x, ng[0:1], mod).astype(BF16)
    for jb in range(d // COL_BLOCK):
        cols = slice(jb * COL_BLOCK, (jb + 1) * COL_BLOCK)
        c_gate = _dot(h, win_ref[:, d + jb * COL_BLOCK:d + (jb + 1) * COL_BLOCK])
        v = _dot(h, win_ref[:, 2 * d + jb * COL_BLOCK:2 * d + (jb + 1) * COL_BLOCK])
        _store_col_tiles(cv_ref, jb * (COL_BLOCK // LANES), c_gate * v)
        b_gate = _dot(h, win_ref[:, cols])
        for j in range(jb * (COL_BLOCK // LANES), (jb + 1) * (COL_BLOCK // LANES)):
            lanes = slice(j * LANES, (j + 1) * LANES)
            u = _causal_conv_tile(cv_ref, j, cw_ref[:, lanes], tm)
            _carry_halo(cv_ref, j, tm)
            bu_ref[:, lanes] = (b_gate[:, lanes.start - cols.start:lanes.stop - cols.start]
                                * u).astype(BF16)
    y = _dot(bu_ref[...], wout_ref[:, 0:d])
    o_ref[...] = _residual(x, y, ng[1:2], mod)


def _conv_mixer_call(x, mod, ng, win, cw, wout, sub, seq, tm):
    tokens, d = x.shape
    tiles_per_seq = seq // tm
    return pl.pallas_call(
        functools.partial(_conv_mixer_kernel, tiles_per_seq=tiles_per_seq),
        grid=(tokens // tm,),
        in_specs=[
            pl.BlockSpec((tm, d), lambda t: (t, 0)),
            _mod_spec(d, tiles_per_seq, sub),
            _const_spec((4, d)),
            _const_spec(win.shape),
            _const_spec(cw.shape),
            _const_spec(wout.shape),
        ],
        out_specs=pl.BlockSpec((tm, d), lambda t: (t, 0)),
        out_shape=jax.ShapeDtypeStruct((tokens, d), F32),
        scratch_shapes=[
            pltpu.VMEM((d // LANES, HALO + tm, LANES), F32),
            pltpu.VMEM((tm, d), BF16),
        ],
        compiler_params=pltpu.CompilerParams(
            dimension_semantics=("arbitrary",), vmem_limit_bytes=VMEM_LIMIT),
        name="conv_mixer",
    )(x, mod, ng, win, cw, wout)


def _ssd_mixer_kernel(x_ref, mod_ref, ng_ref, win_ref, wdt_ref, cw_ref, cb_ref,
                      dtb_ref, alog_ref, dskip_ref, mg_ref, expand_ref, wout_ref, o_ref,
                      raw_ref, state_ref, xbc_ref, z_ref, dt_ref, y_ref, *, tiles_per_seq):
    tm, d = x_ref.shape
    d_inner = z_ref.shape[1]
    conv_dim = xbc_ref.shape[1]
    gn = N_GROUPS * D_STATE
    lc = SSD_CHUNK

    @pl.when(pl.program_id(0) % tiles_per_seq == 0)
    def _():
        raw_ref[:, 0:HALO, :] = jnp.zeros((raw_ref.shape[0], HALO, LANES), F32)
        state_ref[...] = jnp.zeros_like(state_ref)

    x = x_ref[...]
    mod = mod_ref[0, 0]
    ng = ng_ref[...]
    h = _prenorm(x, ng[0:1], mod).astype(BF16)

    head_lane = lax.broadcasted_iota(jnp.int32, (1, LANES), 1) < N_HEADS
    dt_raw = _dot(h, wdt_ref[...]) + dtb_ref[...]
    softplus = jnp.maximum(dt_raw, 0.0) + jnp.log1p(jnp.exp(-jnp.abs(dt_raw)))
    dt_ref[...] = jnp.where(head_lane, softplus, 0.0)
    for jb in range(d_inner // COL_BLOCK):
        cols = slice(jb * COL_BLOCK, (jb + 1) * COL_BLOCK)
        z_ref[:, cols] = _dot(h, win_ref[:, cols])
    tiles_per_block = COL_BLOCK // LANES
    for jb in range(conv_dim // COL_BLOCK):
        raw = _dot(h, win_ref[:, d_inner + jb * COL_BLOCK:d_inner + (jb + 1) * COL_BLOCK])
        _store_col_tiles(raw_ref, jb * tiles_per_block, raw)
        for j in range(jb * tiles_per_block, (jb + 1) * tiles_per_block):
            lanes = slice(j * LANES, (j + 1) * LANES)
            conv = _causal_conv_tile(raw_ref, j, cw_ref[:, lanes], tm) + cb_ref[:, lanes]
            _carry_halo(raw_ref, j, tm)
            xbc_ref[:, lanes] = _silu(conv)

    a2 = jnp.where(head_lane, -jnp.exp(alog_ref[...]) * LOG2E, 0.0)
    row = lax.broadcasted_iota(jnp.int32, (lc, lc), 0)
    col = lax.broadcasted_iota(jnp.int32, (lc, lc), 1)
    tril = (row >= col).astype(BF16)
    tril3 = jnp.concatenate([tril, tril, tril], axis=1)
    row_w = lax.broadcasted_iota(jnp.int32, (lc, d_inner), 0)
    pos_w = lax.broadcasted_iota(jnp.int32, (lc, d_inner), 1) % HEAD_DIM
    diag_w = row_w == pos_w
    causal_w = row_w >= pos_w
    head_of_lane = lax.broadcasted_iota(jnp.int32, (lc, GROUP_WIDTH), 1) // HEAD_DIM
    expand = expand_ref[...]
    dskip = dskip_ref[...]
    mg = mg_ref[...]

    def lane_pack(v):
        hi, mid, lo = _split3(v)
        return hi + pltpu.roll(mid, N_HEADS, axis=1) + pltpu.roll(lo, 2 * N_HEADS, axis=1)

    for c in range(tm // lc):
        rows = slice(c * lc, (c + 1) * lc)
        xs = xbc_ref[rows, 0:d_inner]
        bm = xbc_ref[rows, d_inner:d_inner + gn].astype(BF16)
        cm = xbc_ref[rows, d_inner + gn:d_inner + 2 * gn].astype(BF16)
        dt = dt_ref[rows, :]
        da = dt * a2
        cs = _dot(tril3, jnp.concatenate(_split3(da), axis=0).astype(BF16))
        packed = jnp.concatenate([lane_pack(cs), lane_pack(dt)], axis=0).astype(BF16)
        wide = _dot(packed, expand)
        cs_w = wide[0:lc]
        dt_w = wide[lc:2 * lc]
        cs_last = cs_w[lc - 1:lc]
        xdt = xs * dt_w
        ecs = jnp.exp2(cs_w)
        to_end = (xdt * jnp.exp2(cs_last - cs_w)).astype(BF16)
        chunk_decay = jnp.exp2(cs_last)
        cs_row = jnp.sum(jnp.where(diag_w, cs_w, 0.0), axis=0, keepdims=True)
        decay = jnp.exp2(jnp.where(causal_w, cs_w - cs_row, -jnp.inf))
        for g in range(N_GROUPS):
            gl = slice(g * GROUP_WIDTH, (g + 1) * GROUP_WIDTH)
            nl = slice(g * D_STATE, (g + 1) * D_STATE)
            bm_g = bm[:, nl]
            cm_g = cm[:, nl]
            b4 = jnp.concatenate([bm_g] * HEADS_PER_GROUP, axis=0)
            cb4 = lax.dot_general(cm_g, b4, (((1,), (1,)), ((), ())),
                                  preferred_element_type=F32)
            m_g = (cb4 * decay[:, gl]).astype(BF16)
            xdt_g = xdt[:, gl]
            blockdiag = jnp.concatenate(
                [jnp.where(head_of_lane == r, xdt_g, 0.0) for r in range(HEADS_PER_GROUP)],
                axis=0).astype(BF16)
            state_g = state_ref[:, gl]
            y_g = _dot(m_g, blockdiag) + ecs[:, gl] * _dot(cm_g, state_g.astype(BF16))
            new_state = lax.dot_general(bm_g, to_end[:, gl], (((0,), (0,)), ((), ())),
                                        preferred_element_type=F32)
            state_ref[:, gl] = state_g * chunk_decay[:, gl] + new_state
            y_g = (y_g + dskip[:, gl] * xs[:, gl]) * _silu(z_ref[rows, gl])
            y_ref[rows, gl] = (_rms_normalize(y_g) * mg[:, gl]).astype(BF16)

    y = _dot(y_ref[...], wout_ref[:, 0:d])
    o_ref[...] = _residual(x, y, ng[1:2], mod)


def _ssd_mixer_call(x, mod, ng, win, wdt, cw, cb, dtb, alog, dskip, mg, expand, wout,
                    sub, seq, tm):
    tokens, d = x.shape
    d_inner = wout.shape[0]
    conv_dim = cw.shape[1]
    tiles_per_seq = seq // tm
    consts = (ng, win, wdt, cw, cb, dtb, alog, dskip, mg, expand, wout)
    return pl.pallas_call(
        functools.partial(_ssd_mixer_kernel, tiles_per_seq=tiles_per_seq),
        grid=(tokens // tm,),
        in_specs=[pl.BlockSpec((tm, d), lambda t: (t, 0)), _mod_spec(d, tiles_per_seq, sub)]
        + [_const_spec(a.shape) for a in consts],
        out_specs=pl.BlockSpec((tm, d), lambda t: (t, 0)),
        out_shape=jax.ShapeDtypeStruct((tokens, d), F32),
        scratch_shapes=[
            pltpu.VMEM((conv_dim // LANES, HALO + tm, LANES), F32),
            pltpu.VMEM((D_STATE, d_inner), F32),
            pltpu.VMEM((tm, conv_dim), F32),
            pltpu.VMEM((tm, d_inner), F32),
            pltpu.VMEM((tm, LANES), F32),
            pltpu.VMEM((tm, d_inner), BF16),
        ],
        compiler_params=pltpu.CompilerParams(
            dimension_semantics=("arbitrary",), vmem_limit_bytes=VMEM_LIMIT),
        name="ssd_mixer",
    )(x, mod, *consts)


def _pad_lanes(a):
    return jnp.pad(a, ((0, 0), (0, LANES - a.shape[1])))


def kernel(x, c, ada_w, ada_b, norm_g, a_w_in, a_conv_w, a_w_out, m_w_in, m_conv_w, m_conv_b,
           m_dt_bias, m_a_log, m_d, m_norm_g, m_w_out, f_w_in, f_w_out):
    bsz, seq, d = x.shape
    depth = norm_g.shape[0]
    d_inner = m_w_out.shape[1]
    conv_dim = m_conv_w.shape[2]
    tm_ffn = min(512, seq)
    tm_conv = min(512, seq)
    tm_ssd = min(512, seq)

    mod = _mod_call(c, ada_w, ada_b).reshape(bsz, depth * 2, 3, d)
    head_of_col = jnp.arange(d_inner) // HEAD_DIM
    k = jnp.arange(LANES)
    expand = ((k[:, None] % N_HEADS == head_of_col[None, :]) & (k[:, None] < 3 * N_HEADS)
              ).astype(BF16)

    xt = x.reshape(bsz * seq, d)
    for i in range(depth):
        j = i // 2
        if i % 2 == 0:
            xt = _conv_mixer_call(xt, mod, norm_g[i], _pad_cols(a_w_in[j]), a_conv_w[j],
                                  _pad_cols(a_w_out[j]), 2 * i, seq, tm_conv)
        else:
            w_in = m_w_in[j]
            xt = _ssd_mixer_call(
                xt, mod, norm_g[i], w_in.astype(BF16),
                _pad_lanes(w_in[:, d_inner + conv_dim:]).astype(BF16),
                m_conv_w[j], m_conv_b[j][None, :],
                _pad_lanes(m_dt_bias[j][None, :]), _pad_lanes(m_a_log[j][None, :]),
                jnp.repeat(m_d[j], HEAD_DIM)[None, :], m_norm_g[j][None, :],
                expand, _pad_cols(m_w_out[j]), 2 * i, seq, tm_ssd)
        xt = _ffn_call(xt, mod, norm_g[i], f_w_in[i].astype(BF16), _pad_cols(f_w_out[i]),
                       2 * i + 1, seq, tm_ffn)
    return xt.reshape(bsz, seq, d)
```
